```python
import jax, jax.numpy as jnp
from jax import lax
import numpy as np

D_MODEL = 1024
BATCH = 16
SEQ = 4096
DEPTH = 4
DEC_BATCH = 16
DEC_SEQ = 64
PAST_LEN = 1024

CHUNK = 64
N_HEADS = 16
HEAD_DIM = D_MODEL // N_HEADS
LEFT_CHUNKS = 8
WINDOW = LEFT_CHUNKS * CHUNK
BAND = (LEFT_CHUNKS + 1) * CHUNK
REL_CLIP = 2 * CHUNK
N_REL = 2 * REL_CLIP + 1
CONV_W = 3
D_FF = 2816
N_CONV = (DEPTH + 1) // 2
N_ATTN = DEPTH // 2
EPS = 1e-6
ATTN_SCALE = HEAD_DIM ** -0.5

kernel_name = "streaming_conv_chunkattn_macaron"


def rmsnorm(x, g):
    xf = x.astype(jnp.float32)
    y = xf * lax.rsqrt(jnp.mean(xf * xf, axis=-1, keepdims=True) + EPS)
    return (y * g.astype(jnp.float32)).astype(x.dtype)


def swiglu(x, w_gate, w_up, w_down):
    return (jax.nn.silu(x @ w_gate) * (x @ w_up)) @ w_down


def depthwise_causal(u_ext, w_dw):
    T = u_ext.shape[1] - (CONV_W - 1)
    return sum(w_dw[k] * u_ext[:, k:k + T] for k in range(CONV_W))


def conv_mixer_prompt(x, w_in, w_dw, w_out):
    b, c, h = jnp.split(x @ w_in, 3, axis=-1)
    u = c * h
    y = depthwise_causal(jnp.pad(u, ((0, 0), (CONV_W - 1, 0), (0, 0))), w_dw)
    return (b * y) @ w_out, u[:, -(CONV_W - 1):]


def conv_mixer_sample(x, conv_state, w_in, w_dw, w_out):
    b, c, h = jnp.split(x @ w_in, 3, axis=-1)
    u_ext = jnp.concatenate([conv_state.astype(x.dtype), c * h], axis=1)
    y = depthwise_causal(u_ext, w_dw)
    return (b * y) @ w_out, u_ext[:, -(CONV_W - 1):]


def split_heads(x, w_qkv):
    bsz, t, _ = x.shape
    q, k, v = jnp.split(x @ w_qkv, 3, axis=-1)
    shp = (bsz, t, N_HEADS, HEAD_DIM)
    return q.reshape(shp), k.reshape(shp), v.reshape(shp)


def rel_bias_lookup(rel_bias, rel):
    return rel_bias[:, jnp.clip(rel, -REL_CLIP, REL_CLIP) + REL_CLIP].astype(jnp.float32)


def attn_mixer_prompt(x, w_qkv, w_o, rel_bias):
    bsz, S, _ = x.shape
    nc = S // CHUNK
    q, k, v = split_heads(x, w_qkv)
    pad_rows = LEFT_CHUNKS * CHUNK
    kp = jnp.pad(k, ((0, 0), (pad_rows, 0), (0, 0), (0, 0)))
    vp = jnp.pad(v, ((0, 0), (pad_rows, 0), (0, 0), (0, 0)))
    qc = jnp.moveaxis(q.reshape(bsz, nc, CHUNK, N_HEADS, HEAD_DIM), 1, 0)
    i = jnp.arange(CHUNK)
    j = jnp.arange(BAND)
    bias = rel_bias_lookup(rel_bias, i[:, None] + pad_rows - j[None, :])

    def one_chunk(args):
        c, qb = args
        kb = lax.dynamic_slice_in_dim(kp, c * CHUNK, BAND, axis=1)
        vb = lax.dynamic_slice_in_dim(vp, c * CHUNK, BAND, axis=1)
        s = jnp.einsum('bqhd,bkhd->bhqk', qb, kb).astype(jnp.float32) * ATTN_SCALE + bias[None]
        valid = (c * CHUNK - pad_rows + j) >= 0
        s = jnp.where(valid[None, None, None, :], s, -jnp.inf)
        p = jax.nn.softmax(s, axis=-1).astype(vb.dtype)
        return jnp.einsum('bhqk,bkhd->bqhd', p, vb)

    o = lax.map(one_chunk, (jnp.arange(nc), qc))
    o = jnp.moveaxis(o, 0, 1).reshape(bsz, S, D_MODEL)
    keep = min(WINDOW, S)
    return o @ w_o, k[:, S - keep:], v[:, S - keep:]


def attn_mixer_sample(x, k_cache, v_cache, w_qkv, w_o, rel_bias):
    bsz, T, _ = x.shape
    R = k_cache.shape[1]
    q, k, v = split_heads(x, w_qkv)
    kk = jnp.concatenate([k_cache.astype(k.dtype), k], axis=1)
    vv = jnp.concatenate([v_cache.astype(v.dtype), v], axis=1)
    qpos = PAST_LEN + jnp.arange(T)
    kpos = jnp.concatenate([PAST_LEN - R + jnp.arange(R), PAST_LEN + jnp.arange(T)])
    bias = rel_bias_lookup(rel_bias, qpos[:, None] - kpos[None, :])
    qc, kc = qpos // CHUNK, kpos // CHUNK
    valid = (kc[None, :] <= qc[:, None]) & (kc[None, :] >= qc[:, None] - LEFT_CHUNKS)
    s = jnp.einsum('bqhd,bkhd->bhqk', q, kk).astype(jnp.float32) * ATTN_SCALE + bias[None]
    s = jnp.where(valid[None, None], s, -jnp.inf)
    p = jax.nn.softmax(s, axis=-1).astype(vv.dtype)
    o = jnp.einsum('bhqk,bkhd->bqhd', p, vv).reshape(bsz, T, D_MODEL)
    return o @ w_o, k, v


def setup_inputs(seed: int = 0) -> dict:
    key = jax.random.key(seed)
    ks = jax.random.split(key, 24)
    cache_rows = min(WINDOW, PAST_LEN)

    def nrm(k, shape, s=1.0):
        return jax.random.normal(k, shape, jnp.float32) * s

    d, f = D_MODEL, D_FF
    return {
        "x_prompt": nrm(ks[0], (BATCH, SEQ, d)),
        "x_sample": nrm(ks[1], (DEC_BATCH, DEC_SEQ, d)),
        "state_conv": nrm(ks[2], (N_CONV, DEC_BATCH, CONV_W - 1, d)),
        "cache_k": nrm(ks[3], (N_ATTN, DEC_BATCH, cache_rows, N_HEADS, HEAD_DIM)),
        "cache_v": nrm(ks[4], (N_ATTN, DEC_BATCH, cache_rows, N_HEADS, HEAD_DIM)),
        "ffn1_norm": 1.0 + nrm(ks[5], (DEPTH, d), 0.01),
        "ffn1_w_gate": nrm(ks[6], (DEPTH, d, f), d ** -0.5),
        "ffn1_w_up": nrm(ks[7], (DEPTH, d, f), d ** -0.5),
        "ffn1_w_down": nrm(ks[8], (DEPTH, f, d), f ** -0.5),
        "mix_norm": 1.0 + nrm(ks[9], (DEPTH, d), 0.01),
        "conv_w_in": nrm(ks[10], (N_CONV, d, 3 * d), d ** -0.5),
        "conv_w_dw": nrm(ks[11], (N_CONV, CONV_W, d), CONV_W ** -0.5),
        "conv_w_out": nrm(ks[12], (N_CONV, d, d), d ** -0.5),
        "attn_w_qkv": nrm(ks[13], (N_ATTN, d, 3 * d), d ** -0.5),
        "attn_w_o": nrm(ks[14], (N_ATTN, d, d), d ** -0.5),
        "attn_rel_bias": nrm(ks[15], (N_ATTN, N_HEADS, N_REL), 0.5),
        "ffn2_norm": 1.0 + nrm(ks[16], (DEPTH, d), 0.01),
        "ffn2_w_gate": nrm(ks[17], (DEPTH, d, f), d ** -0.5),
        "ffn2_w_up": nrm(ks[18], (DEPTH, d, f), d ** -0.5),
        "ffn2_w_down": nrm(ks[19], (DEPTH, f, d), f ** -0.5),
        "final_norm": 1.0 + nrm(ks[20], (d,), 0.01),
    }


def reference(x_prompt, x_sample, state_conv, cache_k, cache_v,
              ffn1_norm, ffn1_w_gate, ffn1_w_up, ffn1_w_down,
              mix_norm, conv_w_in, conv_w_dw, conv_w_out,
              attn_w_qkv, attn_w_o, attn_rel_bias,
              ffn2_norm, ffn2_w_gate, ffn2_w_up, ffn2_w_down,
              final_norm):
    xp, xs = x_prompt, x_sample
    conv_p, conv_s, kp_l, vp_l, ks_l, vs_l = [], [], [], [], [], []
    for l in range(DEPTH):
        xp = xp + 0.5 * swiglu(rmsnorm(xp, ffn1_norm[l]), ffn1_w_gate[l], ffn1_w_up[l], ffn1_w_down[l])
        xs = xs + 0.5 * swiglu(rmsnorm(xs, ffn1_norm[l]), ffn1_w_gate[l], ffn1_w_up[l], ffn1_w_down[l])
        hp, hs = rmsnorm(xp, mix_norm[l]), rmsnorm(xs, mix_norm[l])
        j = l // 2
        if l % 2 == 0:
            mp, st_p = conv_mixer_prompt(hp, conv_w_in[j], conv_w_dw[j], conv_w_out[j])
            ms, st_s = conv_mixer_sample(hs, state_conv[j], conv_w_in[j], conv_w_dw[j], conv_w_out[j])
            conv_p.append(st_p)
            conv_s.append(st_s)
        else:
            mp, k_p, v_p = attn_mixer_prompt(hp, attn_w_qkv[j], attn_w_o[j], attn_rel_bias[j])
            ms, k_s, v_s = attn_mixer_sample(hs, cache_k[j], cache_v[j], attn_w_qkv[j], attn_w_o[j], attn_rel_bias[j])
            kp_l.append(k_p)
            vp_l.append(v_p)
            ks_l.append(k_s)
            vs_l.append(v_s)
        xp = xp + mp
        xs = xs + ms
        xp = xp + 0.5 * swiglu(rmsnorm(xp, ffn2_norm[l]), ffn2_w_gate[l], ffn2_w_up[l], ffn2_w_down[l])
        xs = xs + 0.5 * swiglu(rmsnorm(xs, ffn2_norm[l]), ffn2_w_gate[l], ffn2_w_up[l], ffn2_w_down[l])
    y_prompt = rmsnorm(xp, final_norm)
    y_sample = rmsnorm(xs, final_norm)
    new_conv_prompt = jnp.stack(conv_p)
    new_conv_sample = jnp.stack(conv_s)
    new_k_prompt = jnp.stack(kp_l)
    new_v_prompt = jnp.stack(vp_l)
    new_k_sample = jnp.stack(ks_l)
    new_v_sample = jnp.stack(vs_l)
    return (y_prompt, y_sample, new_conv_prompt, new_conv_sample, new_k_prompt, new_v_prompt, new_k_sample, new_v_sample)
```

```python
import functools

import jax
import jax.numpy as jnp
from jax import lax
from jax.experimental import pallas as pl
from jax.experimental.pallas import tpu as pltpu

CHUNK = 64
N_HEADS = 16
LEFT_CHUNKS = 8
WINDOW = LEFT_CHUNKS * CHUNK
BAND = WINDOW + CHUNK
REL_CLIP = 2 * CHUNK
N_REL = 2 * REL_CLIP + 1
CONV_W = 3
PAST_LEN = 1024
EPS = 1e-6

V7X_LANES = 128
V7X_VMEM_LIMIT_BYTES = 56 * 1024 * 1024

F32 = jnp.float32
BF16 = jnp.bfloat16


def _params(n_grid_dims):
    return pltpu.CompilerParams(
        dimension_semantics=("arbitrary",) * n_grid_dims,
        vmem_limit_bytes=V7X_VMEM_LIMIT_BYTES,
    )


def _resident(shape):
    zeros = (0,) * len(shape)
    return pl.BlockSpec(shape, lambda *_: zeros, pipeline_mode=pl.Buffered(1))


def _rmsnorm(x, g):
    y = x * lax.rsqrt(jnp.mean(x * x, axis=-1, keepdims=True) + EPS)
    return y * g


def _ffn_kernel(x_ref, g_ref, wg_ref, wu_ref, wd_ref, gf_ref, o_ref, h_ref, *, f_chunk, final_norm):
    x = x_ref[...]
    xn = _rmsnorm(x, g_ref[...]).astype(BF16)
    d_ff = wg_ref.shape[1]
    for j in range(d_ff // f_chunk):
        sl = slice(j * f_chunk, (j + 1) * f_chunk)
        g = jnp.dot(xn, wg_ref[:, sl], preferred_element_type=F32)
        u = jnp.dot(xn, wu_ref[:, sl], preferred_element_type=F32)
        h_ref[:, sl] = (g * jax.nn.sigmoid(g) * u).astype(BF16)
    y = x + 0.5 * jnp.dot(h_ref[...], wd_ref[...], preferred_element_type=F32)
    if final_norm:
        y = _rmsnorm(y, gf_ref[...])
    o_ref[...] = y


def _ffn(x, g, wg, wu, wd, gf, *, tm, final_norm):
    t, d = x.shape
    d_ff = wg.shape[1]
    return pl.pallas_call(
        functools.partial(_ffn_kernel, f_chunk=256, final_norm=final_norm),
        grid=(t // tm,),
        in_specs=[
            pl.BlockSpec((tm, d), lambda i: (i, 0)),
            _resident((1, d)),
            _resident((d, d_ff)),
            _resident((d, d_ff)),
            _resident((d_ff, d)),
            _resident((1, d)),
        ],
        out_specs=pl.BlockSpec((tm, d), lambda i: (i, 0)),
        out_shape=jax.ShapeDtypeStruct((t, d), F32),
        scratch_shapes=[pltpu.VMEM((tm, d_ff), BF16)],
        compiler_params=_params(1),
        name="ffn",
    )(x, g, wg, wu, wd, gf)


def _conv_kernel(x_ref, st_ref, g_ref, win_ref, wdw_ref, wout_ref, o_ref, nst_ref, carry_ref, *, nb, tt):
    d = x_ref.shape[1]

    @pl.when(pl.program_id(1) == 0)
    def _():
        carry_ref[...] = st_ref[...]

    x = x_ref[...]
    xn = _rmsnorm(x, g_ref[...]).astype(BF16)
    bch = jnp.dot(xn, win_ref[...], preferred_element_type=F32)
    b, c, h = bch[:, :d], bch[:, d:2 * d], bch[:, 2 * d:]
    u = c * h
    u3 = u.reshape(nb, tt, d)
    u1 = pltpu.roll(u, 1, axis=0).reshape(nb, tt, d)
    u2 = pltpu.roll(u, 2, axis=0).reshape(nb, tt, d)
    pos = lax.broadcasted_iota(jnp.int32, (nb, tt, 1), 1)
    prev = carry_ref[...]
    p0, p1 = prev[:, 0:1, :], prev[:, 1:2, :]
    u1 = jnp.where(pos == 0, p1, u1)
    u2 = jnp.where(pos == 0, p0, jnp.where(pos == 1, p1, u2))
    w = wdw_ref[...]
    y = w[0:1, :] * u2 + w[1:2, :] * u1 + w[2:3, :] * u3
    last = u3[:, tt - (CONV_W - 1):, :]
    carry_ref[...] = last
    nst_ref[...] = last
    by = (b.reshape(nb, tt, d) * y).reshape(nb * tt, d).astype(BF16)
    o_ref[...] = x + jnp.dot(by, wout_ref[...], preferred_element_type=F32)


def _conv_mixer(x, state, g, win, wdw, wout, *, bsz, nb, tt):
    rows, d = x.shape
    nt = rows // bsz // tt
    return pl.pallas_call(
        functools.partial(_conv_kernel, nb=nb, tt=tt),
        grid=(bsz // nb, nt),
        in_specs=[
            pl.BlockSpec((nb * tt, d), lambda b, t: (b * nt + t, 0)),
            pl.BlockSpec((nb, CONV_W - 1, d), lambda b, t: (b, 0, 0)),
            _resident((1, d)),
            _resident((d, 3 * d)),
            _resident((CONV_W, d)),
            _resident((d, d)),
        ],
        out_specs=[
            pl.BlockSpec((nb * tt, d), lambda b, t: (b * nt + t, 0)),
            pl.BlockSpec((nb, CONV_W - 1, d), lambda b, t: (b, 0, 0)),
        ],
        out_shape=[
            jax.ShapeDtypeStruct((rows, d), F32),
            jax.ShapeDtypeStruct((bsz, CONV_W - 1, d), F32),
        ],
        scratch_shapes=[pltpu.VMEM((nb, CONV_W - 1, d), F32)],
        compiler_params=_params(2),
        name="conv_mixer",
    )(x, state, g, win, wdw, wout)


def _qkv_kernel(x_ref, g_ref, w_ref, kin_ref, vin_ref, q_ref, kb_ref, vb_ref, kf_ref, vf_ref,
                *, nb, tt, first_kept_tile, scale):
    del kin_ref, vin_ref
    d = x_ref.shape[1]
    xn = _rmsnorm(x_ref[...], g_ref[...]).astype(BF16)
    qkv = jnp.dot(xn, w_ref[...], preferred_element_type=F32)
    q_ref[...] = (qkv[:, :d] * scale).astype(BF16)
    k = qkv[:, d:2 * d].reshape(nb, tt, d)
    v = qkv[:, 2 * d:].reshape(nb, tt, d)
    kb_ref[...] = k.astype(BF16)
    vb_ref[...] = v.astype(BF16)

    @pl.when(pl.program_id(1) >= first_kept_tile)
    def _():
        kf_ref[...] = k
        vf_ref[...] = v


def _qkv(x, g, w, k_band, v_band, *, bsz, nb, tt, keep, scale):
    rows, d = x.shape
    nt = rows // bsz // tt
    pad_tiles = WINDOW // tt
    first_kept_tile = nt - keep // tt
    band_spec = pl.BlockSpec((nb, tt, d), lambda b, t: (b, t + pad_tiles, 0))
    kept_spec = pl.BlockSpec((nb, tt, d), lambda b, t: (b, jnp.maximum(t - first_kept_tile, 0), 0))
    return pl.pallas_call(
        functools.partial(_qkv_kernel, nb=nb, tt=tt, first_kept_tile=first_kept_tile, scale=scale),
        grid=(bsz // nb, nt),
        in_specs=[
            pl.BlockSpec((nb * tt, d), lambda b, t: (b * nt + t, 0)),
            _resident((1, d)),
            _resident((d, 3 * d)),
            pl.BlockSpec(memory_space=pl.ANY),
            pl.BlockSpec(memory_space=pl.ANY),
        ],
        out_specs=[
            pl.BlockSpec((nb * tt, d), lambda b, t: (b * nt + t, 0)),
            band_spec, band_spec, kept_spec, kept_spec,
        ],
        out_shape=[
            jax.ShapeDtypeStruct((rows, d), BF16),
            jax.ShapeDtypeStruct(k_band.shape, BF16),
            jax.ShapeDtypeStruct(v_band.shape, BF16),
            jax.ShapeDtypeStruct((bsz, keep, d), F32),
            jax.ShapeDtypeStruct((bsz, keep, d), F32),
        ],
        input_output_aliases={3: 1, 4: 2},
        compiler_params=_params(2),
        name="qkv",
    )(x, g, w, k_band, v_band)


def _bias_kernel(tab_ref, o_ref, *, var_start):
    h = pl.program_id(0)
    var_w = 2 * V7X_LANES
    ii = lax.broadcasted_iota(jnp.int32, (CHUNK, var_w), 0)
    jj = lax.broadcasted_iota(jnp.int32, (CHUNK, var_w), 1) + var_start
    idx = jnp.clip(ii + WINDOW - jj, -REL_CLIP, REL_CLIP) + REL_CLIP

    def body(r, acc):
        return jnp.where(idx == r, tab_ref[h, r], acc)

    acc = lax.fori_loop(0, N_REL, body, jnp.zeros((CHUNK, var_w), F32))
    o_ref[:, :var_start] = jnp.full((CHUNK, var_start), tab_ref[h, N_REL - 1], F32)
    o_ref[:, var_start:] = acc[:, :BAND - var_start]


def _rel_bias_band(table):
    n_heads = table.shape[0]
    var_start = WINDOW - REL_CLIP
    assert var_start % V7X_LANES == 0 and BAND - var_start <= 2 * V7X_LANES
    return pl.pallas_call(
        functools.partial(_bias_kernel, var_start=var_start),
        grid=(n_heads,),
        in_specs=[pl.BlockSpec(memory_space=pltpu.SMEM)],
        out_specs=pl.BlockSpec((None, CHUNK, BAND), lambda h: (h, 0, 0)),
        out_shape=jax.ShapeDtypeStruct((n_heads, CHUNK, BAND), F32),
        compiler_params=_params(1),
        name="rel_bias",
    )(table)


def _block_bias(band, qb):
    n_q = qb // CHUNK
    neg = -jnp.inf
    rows = [jnp.pad(band, ((0, 0), (0, 0), (c * CHUNK, (n_q - 1 - c) * CHUNK)), constant_values=neg)
            for c in range(n_q)]
    return jnp.concatenate(rows, axis=1)


def _attn_kernel(q_ref, k_ref, v_ref, bias_ref, o_ref, *, qb, kw, n_pairs, mask_before_start):
    i = pl.program_id(2)
    start = pl.multiple_of(i * qb, qb)
    k_win = k_ref[pl.ds(start, kw), :]
    v_win = v_ref[pl.ds(start, kw), :]
    half = V7X_LANES // 2
    low = lax.broadcasted_iota(jnp.int32, (1, V7X_LANES), 1) < half
    if mask_before_start:
        col = lax.broadcasted_iota(jnp.int32, (1, kw), 1)
        key_exists = col >= WINDOW - i * qb
    for p in range(n_pairs):
        lanes = slice(p * V7X_LANES, (p + 1) * V7X_LANES)
        q = q_ref[:, lanes].astype(F32)
        kp = k_win[:, lanes]
        vp = v_win[:, lanes]
        halves = []
        for hh in range(2):
            qh = jnp.where(low if hh == 0 else jnp.logical_not(low), q, 0.0).astype(BF16)
            s = lax.dot_general(qh, kp, (((1,), (1,)), ((), ())), preferred_element_type=F32)
            s = s + bias_ref[2 * p + hh]
            if mask_before_start:
                s = jnp.where(key_exists, s, -jnp.inf)
            m = jnp.max(s, axis=-1, keepdims=True)
            e = jnp.exp(s - m)
            l = jnp.sum(e, axis=-1, keepdims=True)
            halves.append(jnp.dot(e.astype(BF16), vp, preferred_element_type=F32) / l)
        o_ref[:, lanes] = jnp.where(low, halves[0], halves[1]).astype(BF16)


def _attention(q, k_band, v_band, bias, *, bsz, qb, n_pairs, mask_before_start):
    rows, d = q.shape
    t = rows // bsz
    nqb = t // qb
    kw = WINDOW + qb
    gl = n_pairs * V7X_LANES
    return pl.pallas_call(
        functools.partial(_attn_kernel, qb=qb, kw=kw, n_pairs=n_pairs, mask_before_start=mask_before_start),
        grid=(bsz, d // gl, nqb),
        in_specs=[
            pl.BlockSpec((qb, gl), lambda b, g, i: (b * nqb + i, g)),
            pl.BlockSpec((None, WINDOW + t, gl), lambda b, g, i: (b, 0, g)),
            pl.BlockSpec((None, WINDOW + t, gl), lambda b, g, i: (b, 0, g)),
            pl.BlockSpec((2 * n_pairs, qb, kw), lambda b, g, i: (g, 0, 0)),
        ],
        out_specs=pl.BlockSpec((qb, gl), lambda b, g, i: (b * nqb + i, g)),
        out_shape=jax.ShapeDtypeStruct((rows, d), BF16),
        compiler_params=_params(3),
        name="attention",
    )(q, k_band, v_band, bias)


def _oproj_kernel(x_ref, o_ref, w_ref, out_ref):
    out_ref[...] = x_ref[...] + jnp.dot(o_ref[...], w_ref[...], preferred_element_type=F32)


def _oproj(x, o, w, *, tm):
    t, d = x.shape
    return pl.pallas_call(
        _oproj_kernel,
        grid=(t // tm,),
        in_specs=[
            pl.BlockSpec((tm, d), lambda i: (i, 0)),
            pl.BlockSpec((tm, d), lambda i: (i, 0)),
            _resident((d, d)),
        ],
        out_specs=pl.BlockSpec((tm, d), lambda i: (i, 0)),
        out_shape=jax.ShapeDtypeStruct((t, d), F32),
        compiler_params=_params(1),
        name="oproj",
    )(x, o, w)


def _attn_mixer(x, g, wqkv, wo, bias, k_band, v_band, *, bsz, nb, tt, keep, qb, n_pairs, mask_before_start):
    d = x.shape[1]
    scale = (d // N_HEADS) ** -0.5
    q, k_band, v_band, k_new, v_new = _qkv(x, g, wqkv, k_band, v_band, bsz=bsz, nb=nb, tt=tt, keep=keep, scale=scale)
    o = _attention(q, k_band, v_band, bias, bsz=bsz, qb=qb, n_pairs=n_pairs, mask_before_start=mask_before_start)
    return _oproj(x, o, wo, tm=nb * tt), k_new, v_new


def kernel(x_prompt, x_sample, state_conv, cache_k, cache_v, ffn1_norm, ffn1_w_gate, ffn1_w_up, ffn1_w_down, mix_norm, conv_w_in, conv_w_dw, conv_w_out, attn_w_qkv, attn_w_o, attn_rel_bias, ffn2_norm, ffn2_w_gate, ffn2_w_up, ffn2_w_down, final_norm):
    bp, sp, d = x_prompt.shape
    bs, ss, _ = x_sample.shape
    depth = ffn1_norm.shape[0]
    head_dim = d // N_HEADS
    cache_rows = cache_k.shape[2]
    assert ss == CHUNK and cache_rows == WINDOW and PAST_LEN % CHUNK == 0
    assert sp % WINDOW == 0 and sp >= WINDOW and 2 * head_dim == V7X_LANES

    tm = 512
    qb = 256
    xp = x_prompt.reshape(bp * sp, d)
    xs = x_sample.reshape(bs * ss, d)
    row = lambda a: a.reshape(1, d)
    gf = row(final_norm)

    def ffn_both(xp, xs, g, wg, wu, wd, final):
        g, wg, wu, wd = row(g), wg.astype(BF16), wu.astype(BF16), wd.astype(BF16)
        xp = _ffn(xp, g, wg, wu, wd, gf, tm=tm, final_norm=final)
        xs = _ffn(xs, g, wg, wu, wd, gf, tm=min(tm, xs.shape[0]), final_norm=final)
        return xp, xs

    conv_p, conv_s, kp_l, vp_l, ks_l, vs_l = [], [], [], [], [], []
    for l in range(depth):
        xp, xs = ffn_both(xp, xs, ffn1_norm[l], ffn1_w_gate[l], ffn1_w_up[l], ffn1_w_down[l], False)
        j = l // 2
        g = row(mix_norm[l])
        if l % 2 == 0:
            win, wdw, wout = conv_w_in[j].astype(BF16), conv_w_dw[j], conv_w_out[j].astype(BF16)
            zero_state = jnp.zeros((bp, CONV_W - 1, d), F32)
            xp, st_p = _conv_mixer(xp, zero_state, g, win, wdw, wout, bsz=bp, nb=1, tt=tm)
            xs, st_s = _conv_mixer(xs, state_conv[j], g, win, wdw, wout, bsz=bs, nb=bs, tt=ss)
            conv_p.append(st_p)
            conv_s.append(st_s)
        else:
            wqkv, wo = attn_w_qkv[j].astype(BF16), attn_w_o[j].astype(BF16)
            band = _rel_bias_band(attn_rel_bias[j])
            zeros_p = jnp.zeros((bp, WINDOW + sp, d), BF16)
            xp, k_p, v_p = _attn_mixer(
                xp, g, wqkv, wo, _block_bias(band, qb), zeros_p, zeros_p,
                bsz=bp, nb=1, tt=tm, keep=WINDOW, qb=qb, n_pairs=2, mask_before_start=True)
            new_rows = jnp.zeros((bs, ss, d), BF16)
            k_band = jnp.concatenate([cache_k[j].reshape(bs, cache_rows, d).astype(BF16), new_rows], axis=1)
            v_band = jnp.concatenate([cache_v[j].reshape(bs, cache_rows, d).astype(BF16), new_rows], axis=1)
            xs, k_s, v_s = _attn_mixer(
                xs, g, wqkv, wo, band, k_band, v_band,
                bsz=bs, nb=bs, tt=ss, keep=ss, qb=ss, n_pairs=2, mask_before_start=False)
            heads = lambda a: a.reshape(a.shape[0], a.shape[1], N_HEADS, head_dim)
            kp_l.append(heads(k_p))
            vp_l.append(heads(v_p))
            ks_l.append(heads(k_s))
            vs_l.append(heads(v_s))
        xp, xs = ffn_both(xp, xs, ffn2_norm[l], ffn2_w_gate[l], ffn2_w_up[l], ffn2_w_down[l], l == depth - 1)
    return (xp.reshape(bp, sp, d), xs.reshape(bs, ss, d),
            jnp.stack(conv_p), jnp.stack(conv_s),
            jnp.stack(kp_l), jnp.stack(vp_l), jnp.stack(ks_l), jnp.stack(vs_l))
```

```python
import functools

import jax
import jax.numpy as jnp
from jax import lax
from jax.experimental import pallas as pl
from jax.experimental.pallas import tpu as pltpu

CHUNK = 64
N_HEADS = 16
LEFT_CHUNKS = 8
WINDOW = LEFT_CHUNKS * CHUNK
BAND = WINDOW + CHUNK
REL_CLIP = 2 * CHUNK
N_REL = 2 * REL_CLIP + 1
CONV_W = 3
PAST_LEN = 1024
EPS = 1e-6
LOG2_E = 1.4426950408889634

V7X_LANES = 128
V7X_VMEM_LIMIT_BYTES = 56 * 1024 * 1024

F32 = jnp.float32
BF16 = jnp.bfloat16


def _params(n_grid_dims):
    return pltpu.CompilerParams(
        dimension_semantics=("arbitrary",) * n_grid_dims,
        vmem_limit_bytes=V7X_VMEM_LIMIT_BYTES,
    )


def _resident(shape):
    zeros = (0,) * len(shape)
    return pl.BlockSpec(shape, lambda *_: zeros, pipeline_mode=pl.Buffered(1))


def _rmsnorm(x, g):
    y = x * lax.rsqrt(jnp.mean(x * x, axis=-1, keepdims=True) + EPS)
    return y * g


def _ffn_kernel(x_ref, g_ref, wg_ref, wu_ref, wd_ref, gf_ref, o_ref, h_ref, *, f_chunk, final_norm):
    x = x_ref[...]
    xn = _rmsnorm(x, g_ref[...]).astype(BF16)
    d_ff = wg_ref.shape[1]
    for j in range(d_ff // f_chunk):
        sl = slice(j * f_chunk, (j + 1) * f_chunk)
        g = jnp.dot(xn, wg_ref[:, sl], preferred_element_type=F32)
        u = jnp.dot(xn, wu_ref[:, sl], preferred_element_type=F32)
        h_ref[:, sl] = (g * jax.nn.sigmoid(g) * u).astype(BF16)
    y = x + 0.5 * jnp.dot(h_ref[...], wd_ref[...], preferred_element_type=F32)
    if final_norm:
        y = _rmsnorm(y, gf_ref[...])
    o_ref[...] = y


def _ffn(x, g, wg, wu, wd, gf, *, tm, final_norm):
    t, d = x.shape
    d_ff = wg.shape[1]
    return pl.pallas_call(
        functools.partial(_ffn_kernel, f_chunk=256, final_norm=final_norm),
        grid=(t // tm,),
        in_specs=[
            pl.BlockSpec((tm, d), lambda i: (i, 0)),
            _resident((1, d)),
            _resident((d, d_ff)),
            _resident((d, d_ff)),
            _resident((d_ff, d)),
            _resident((1, d)),
        ],
        out_specs=pl.BlockSpec((tm, d), lambda i: (i, 0)),
        out_shape=jax.ShapeDtypeStruct((t, d), F32),
        scratch_shapes=[pltpu.VMEM((tm, d_ff), BF16)],
        compiler_params=_params(1),
        name="ffn",
    )(x, g, wg, wu, wd, gf)


def _conv_kernel(x_ref, st_ref, g_ref, win_ref, wdw_ref, wout_ref, o_ref, nst_ref, carry_ref, *, nb, tt):
    d = x_ref.shape[1]

    @pl.when(pl.program_id(1) == 0)
    def _():
        carry_ref[...] = st_ref[...]

    x = x_ref[...]
    xn = _rmsnorm(x, g_ref[...]).astype(BF16)
    bch = jnp.dot(xn, win_ref[...], preferred_element_type=F32)
    b, c, h = bch[:, :d], bch[:, d:2 * d], bch[:, 2 * d:]
    u = c * h
    u3 = u.reshape(nb, tt, d)
    u1 = pltpu.roll(u, 1, axis=0).reshape(nb, tt, d)
    u2 = pltpu.roll(u, 2, axis=0).reshape(nb, tt, d)
    pos = lax.broadcasted_iota(jnp.int32, (nb, tt, 1), 1)
    prev = carry_ref[...]
    p0, p1 = prev[:, 0:1, :], prev[:, 1:2, :]
    u1 = jnp.where(pos == 0, p1, u1)
    u2 = jnp.where(pos == 0, p0, jnp.where(pos == 1, p1, u2))
    w = wdw_ref[...]
    y = w[0:1, :] * u2 + w[1:2, :] * u1 + w[2:3, :] * u3
    last = u3[:, tt - (CONV_W - 1):, :]
    carry_ref[...] = last
    nst_ref[...] = last
    by = (b.reshape(nb, tt, d) * y).reshape(nb * tt, d).astype(BF16)
    o_ref[...] = x + jnp.dot(by, wout_ref[...], preferred_element_type=F32)


def _conv_mixer(x, state, g, win, wdw, wout, *, bsz, nb, tt):
    rows, d = x.shape
    nt = rows // bsz // tt
    return pl.pallas_call(
        functools.partial(_conv_kernel, nb=nb, tt=tt),
        grid=(bsz // nb, nt),
        in_specs=[
            pl.BlockSpec((nb * tt, d), lambda b, t: (b * nt + t, 0)),
            pl.BlockSpec((nb, CONV_W - 1, d), lambda b, t: (b, 0, 0)),
            _resident((1, d)),
            _resident((d, 3 * d)),
            _resident((CONV_W, d)),
            _resident((d, d)),
        ],
        out_specs=[
            pl.BlockSpec((nb * tt, d), lambda b, t: (b * nt + t, 0)),
            pl.BlockSpec((nb, CONV_W - 1, d), lambda b, t: (b, 0, 0)),
        ],
        out_shape=[
            jax.ShapeDtypeStruct((rows, d), F32),
            jax.ShapeDtypeStruct((bsz, CONV_W - 1, d), F32),
        ],
        scratch_shapes=[pltpu.VMEM((nb, CONV_W - 1, d), F32)],
        compiler_params=_params(2),
        name="conv_mixer",
    )(x, state, g, win, wdw, wout)


def _qkv_kernel(x_ref, g_ref, w_ref, kin_ref, vin_ref, q_ref, kb_ref, vb_ref, kf_ref, vf_ref,
                *, nb, tt, first_kept_tile, scale):
    del kin_ref, vin_ref
    d = x_ref.shape[1]
    xn = _rmsnorm(x_ref[...], g_ref[...]).astype(BF16)
    qkv = jnp.dot(xn, w_ref[...], preferred_element_type=F32)
    q = (qkv[:, :d] * scale).astype(BF16).reshape(nb, tt, d)
    k = qkv[:, d:2 * d].reshape(nb, tt, d)
    v = qkv[:, 2 * d:].reshape(nb, tt, d)
    kb = k.astype(BF16)
    vb = v.astype(BF16)
    for p in range(d // V7X_LANES):
        lanes = slice(p * V7X_LANES, (p + 1) * V7X_LANES)
        q_ref[:, p] = q[:, :, lanes]
        kb_ref[:, p] = kb[:, :, lanes]
        vb_ref[:, p] = vb[:, :, lanes]

    @pl.when(pl.program_id(1) >= first_kept_tile)
    def _():
        kf_ref[...] = k
        vf_ref[...] = v


def _qkv(x, g, w, k_band, v_band, *, bsz, nb, tt, keep, scale):
    rows, d = x.shape
    t_len = rows // bsz
    nt = t_len // tt
    n_pairs = d // V7X_LANES
    pad_tiles = WINDOW // tt
    first_kept_tile = nt - keep // tt
    q_spec = pl.BlockSpec((nb, n_pairs, tt, V7X_LANES), lambda b, t: (b, 0, t, 0))
    band_spec = pl.BlockSpec((nb, n_pairs, tt, V7X_LANES), lambda b, t: (b, 0, t + pad_tiles, 0))
    kept_spec = pl.BlockSpec((nb, tt, d), lambda b, t: (b, jnp.maximum(t - first_kept_tile, 0), 0))
    return pl.pallas_call(
        functools.partial(_qkv_kernel, nb=nb, tt=tt, first_kept_tile=first_kept_tile, scale=scale),
        grid=(bsz // nb, nt),
        in_specs=[
            pl.BlockSpec((nb * tt, d), lambda b, t: (b * nt + t, 0)),
            _resident((1, d)),
            _resident((d, 3 * d)),
            pl.BlockSpec(memory_space=pl.ANY),
            pl.BlockSpec(memory_space=pl.ANY),
        ],
        out_specs=[q_spec, band_spec, band_spec, kept_spec, kept_spec],
        out_shape=[
            jax.ShapeDtypeStruct((bsz, n_pairs, t_len, V7X_LANES), BF16),
            jax.ShapeDtypeStruct(k_band.shape, BF16),
            jax.ShapeDtypeStruct(v_band.shape, BF16),
            jax.ShapeDtypeStruct((bsz, keep, d), F32),
            jax.ShapeDtypeStruct((bsz, keep, d), F32),
        ],
        input_output_aliases={3: 1, 4: 2},
        compiler_params=_params(2),
        name="qkv",
    )(x, g, w, k_band, v_band)


def _bias_kernel(tab_ref, o_ref, *, var_start):
    h = pl.program_id(0)
    var_w = 2 * V7X_LANES
    ii = lax.broadcasted_iota(jnp.int32, (CHUNK, var_w), 0)
    jj = lax.broadcasted_iota(jnp.int32, (CHUNK, var_w), 1) + var_start
    idx = jnp.clip(ii + WINDOW - jj, -REL_CLIP, REL_CLIP) + REL_CLIP

    def body(r, acc):
        return jnp.where(idx == r, tab_ref[h, r] * LOG2_E, acc)

    acc = lax.fori_loop(0, N_REL, body, jnp.zeros((CHUNK, var_w), F32))
    o_ref[:, :var_start] = jnp.full((CHUNK, var_start), tab_ref[h, N_REL - 1] * LOG2_E, F32)
    o_ref[:, var_start:] = acc[:, :BAND - var_start]


def _rel_bias_band(table):
    n_heads = table.shape[0]
    var_start = WINDOW - REL_CLIP
    assert var_start % V7X_LANES == 0 and BAND - var_start <= 2 * V7X_LANES
    return pl.pallas_call(
        functools.partial(_bias_kernel, var_start=var_start),
        grid=(n_heads,),
        in_specs=[pl.BlockSpec(memory_space=pltpu.SMEM)],
        out_specs=pl.BlockSpec((None, CHUNK, BAND), lambda h: (h, 0, 0)),
        out_shape=jax.ShapeDtypeStruct((n_heads, CHUNK, BAND), F32),
        compiler_params=_params(1),
        name="rel_bias",
    )(table)


def _block_bias_t(band, rows, n_masked):
    n_heads = band.shape[0]
    n_q = rows // CHUNK
    per_chunk = [jnp.pad(band, ((0, 0), (0, 0), (c * CHUNK, (n_q - 1 - c) * CHUNK)), constant_values=-jnp.inf)
                 for c in range(n_q)]
    b = jnp.concatenate(per_chunk, axis=1)
    b = jnp.swapaxes(b.reshape(n_heads // 2, 2 * rows, WINDOW + rows), 1, 2)
    key = lax.broadcasted_iota(jnp.int32, b.shape, 1)
    return jnp.stack([jnp.where(key >= WINDOW - v * rows, b, -jnp.inf) for v in range(n_masked)] + [b])


def _attn_kernel(q_ref, k_ref, v_ref, bias_ref, o_ref, s0_ref, e0_ref, l0_ref, s1_ref, e1_ref, l1_ref,
                 *, rows, n_pairs):
    win = WINDOW + rows
    n_steps = q_ref.shape[1] // rows
    n_masked = bias_ref.shape[0] - 1
    low = lax.broadcasted_iota(jnp.int32, (1, V7X_LANES), 1) < V7X_LANES // 2

    def scores(t, s_ref):
        r0 = pl.multiple_of(t * rows, rows)
        variant = jnp.minimum(t, n_masked)
        for p in range(n_pairs):
            q = q_ref[p, pl.ds(r0, rows), :]
            zero = jnp.zeros_like(q)
            a = jnp.concatenate([jnp.where(low, q, zero), jnp.where(low, zero, q)], axis=0)
            kw = k_ref[p, pl.ds(r0, win), :]
            s = lax.dot_general(kw, a, (((1,), (1,)), ((), ())), preferred_element_type=F32)
            s_ref[p] = s + bias_ref[variant, p]

    def softmax(s_ref, e_ref, l_ref):
        for p in range(n_pairs):
            m = jnp.max(s_ref[p], axis=0, keepdims=True)
            e = jnp.exp2(s_ref[p] - m)
            l_ref[p] = jnp.sum(e, axis=0, keepdims=True)
            e_ref[p] = e.astype(BF16)

    def values(t, e_ref, l_ref):
        r0 = pl.multiple_of(t * rows, rows)
        for p in range(n_pairs):
            vw = v_ref[p, pl.ds(r0, win), :]
            ot = lax.dot_general(vw, e_ref[p], (((0,), (0,)), ((), ())), preferred_element_type=F32)
            o2 = (ot / l_ref[p]).T
            o_ref[p, pl.ds(r0, rows), :] = jnp.where(low, o2[:rows], o2[rows:]).astype(BF16)

    if n_steps == 1:
        scores(0, s0_ref)
        softmax(s0_ref, e0_ref, l0_ref)
        values(0, e0_ref, l0_ref)
        return
    assert n_steps % 2 == 0
    scores(0, s0_ref)
    scores(1, s1_ref)
    softmax(s0_ref, e0_ref, l0_ref)

    def two_steps(u, carry):
        t = 2 * u
        values(t, e0_ref, l0_ref)
        softmax(s1_ref, e1_ref, l1_ref)
        scores(t + 2, s0_ref)
        values(t + 1, e1_ref, l1_ref)
        softmax(s0_ref, e0_ref, l0_ref)
        scores(t + 3, s1_ref)
        return carry

    lax.fori_loop(0, (n_steps - 2) // 2, two_steps, 0)
    values(n_steps - 2, e0_ref, l0_ref)
    softmax(s1_ref, e1_ref, l1_ref)
    values(n_steps - 1, e1_ref, l1_ref)


def _attention(q, k_band, v_band, bias_t, *, rows, n_pairs):
    bsz, p_total, t_len, lanes = q.shape
    win = WINDOW + rows
    blk = lambda n: pl.BlockSpec((None, n_pairs, n, lanes), lambda b, g: (b, g, 0, 0))
    stage = [pltpu.VMEM((n_pairs, win, 2 * rows), F32), pltpu.VMEM((n_pairs, win, 2 * rows), BF16),
             pltpu.VMEM((n_pairs, 1, 2 * rows), F32)]
    return pl.pallas_call(
        functools.partial(_attn_kernel, rows=rows, n_pairs=n_pairs),
        grid=(bsz, p_total // n_pairs),
        in_specs=[
            blk(t_len), blk(WINDOW + t_len), blk(WINDOW + t_len),
            pl.BlockSpec((bias_t.shape[0], n_pairs, win, 2 * rows), lambda b, g: (0, g, 0, 0)),
        ],
        out_specs=blk(t_len),
        out_shape=jax.ShapeDtypeStruct(q.shape, BF16),
        scratch_shapes=stage + stage,
        compiler_params=_params(2),
        name="attention",
    )(q, k_band, v_band, bias_t)


def _oproj_kernel(x_ref, o_ref, w_ref, out_ref):
    nb, n_pairs, tt, lanes = o_ref.shape
    o = jnp.concatenate([o_ref[:, p] for p in range(n_pairs)], axis=-1).reshape(nb * tt, n_pairs * lanes)
    out_ref[...] = x_ref[...] + jnp.dot(o, w_ref[...], preferred_element_type=F32)


def _oproj(x, o, w, *, nb, tt):
    rows, d = x.shape
    bsz, n_pairs, t_len, lanes = o.shape
    nt = t_len // tt
    return pl.pallas_call(
        _oproj_kernel,
        grid=(bsz // nb, nt),
        in_specs=[
            pl.BlockSpec((nb * tt, d), lambda b, t: (b * nt + t, 0)),
            pl.BlockSpec((nb, n_pairs, tt, lanes), lambda b, t: (b, 0, t, 0)),
            _resident((d, d)),
        ],
        out_specs=pl.BlockSpec((nb * tt, d), lambda b, t: (b * nt + t, 0)),
        out_shape=jax.ShapeDtypeStruct((rows, d), F32),
        compiler_params=_params(2),
        name="oproj",
    )(x, o, w)


def _attn_mixer(x, g, wqkv, wo, band, k_band, v_band, *, bsz, nb, tt, keep, rows, n_pairs, mask_before_start):
    d = x.shape[1]
    scale = (d // N_HEADS) ** -0.5 * LOG2_E
    bias_t = _block_bias_t(band, rows, WINDOW // rows if mask_before_start else 0)
    q, k_band, v_band, k_new, v_new = _qkv(x, g, wqkv, k_band, v_band, bsz=bsz, nb=nb, tt=tt, keep=keep, scale=scale)
    o = _attention(q, k_band, v_band, bias_t, rows=rows, n_pairs=n_pairs)
    return _oproj(x, o, wo, nb=nb, tt=tt), k_new, v_new


def kernel(x_prompt, x_sample, state_conv, cache_k, cache_v, ffn1_norm, ffn1_w_gate, ffn1_w_up, ffn1_w_down, mix_norm, conv_w_in, conv_w_dw, conv_w_out, attn_w_qkv, attn_w_o, attn_rel_bias, ffn2_norm, ffn2_w_gate, ffn2_w_up, ffn2_w_down, final_norm):
    bp, sp, d = x_prompt.shape
    bs, ss, _ = x_sample.shape
    depth = ffn1_norm.shape[0]
    head_dim = d // N_HEADS
    n_pairs = d // V7X_LANES
    cache_rows = cache_k.shape[2]
    assert ss == CHUNK and cache_rows == WINDOW and PAST_LEN % CHUNK == 0
    assert sp % WINDOW == 0 and sp >= WINDOW and 2 * head_dim == V7X_LANES

    tm = 512
    rows_p = 2 * CHUNK
    xp = x_prompt.reshape(bp * sp, d)
    xs = x_sample.reshape(bs * ss, d)
    row = lambda a: a.reshape(1, d)
    gf = row(final_norm)

    def ffn_both(xp, xs, g, wg, wu, wd, final):
        g, wg, wu, wd = row(g), wg.astype(BF16), wu.astype(BF16), wd.astype(BF16)
        xp = _ffn(xp, g, wg, wu, wd, gf, tm=tm, final_norm=final)
        xs = _ffn(xs, g, wg, wu, wd, gf, tm=min(tm, xs.shape[0]), final_norm=final)
        return xp, xs

    def pair_major(a):
        return jnp.swapaxes(a.reshape(a.shape[0], a.shape[1], n_pairs, V7X_LANES), 1, 2)

    conv_p, conv_s, kp_l, vp_l, ks_l, vs_l = [], [], [], [], [], []
    for l in range(depth):
        xp, xs = ffn_both(xp, xs, ffn1_norm[l], ffn1_w_gate[l], ffn1_w_up[l], ffn1_w_down[l], False)
        j = l // 2
        g = row(mix_norm[l])
        if l % 2 == 0:
            win, wdw, wout = conv_w_in[j].astype(BF16), conv_w_dw[j], conv_w_out[j].astype(BF16)
            zero_state = jnp.zeros((bp, CONV_W - 1, d), F32)
            xp, st_p = _conv_mixer(xp, zero_state, g, win, wdw, wout, bsz=bp, nb=1, tt=tm)
            xs, st_s = _conv_mixer(xs, state_conv[j], g, win, wdw, wout, bsz=bs, nb=bs, tt=ss)
            conv_p.append(st_p)
            conv_s.append(st_s)
        else:
            wqkv, wo = attn_w_qkv[j].astype(BF16), attn_w_o[j].astype(BF16)
            band = _rel_bias_band(attn_rel_bias[j])
            zeros_p = jnp.zeros((bp, n_pairs, WINDOW + sp, V7X_LANES), BF16)
            xp, k_p, v_p = _attn_mixer(
                xp, g, wqkv, wo, band, zeros_p, zeros_p,
                bsz=bp, nb=1, tt=tm, keep=WINDOW, rows=rows_p, n_pairs=2, mask_before_start=True)
            new_rows = jnp.zeros((bs, ss, d), BF16)
            with_cache = lambda c: pair_major(jnp.concatenate(
                [c.reshape(bs, cache_rows, d).astype(BF16), new_rows], axis=1))
            xs, k_s, v_s = _attn_mixer(
                xs, g, wqkv, wo, band, with_cache(cache_k[j]), with_cache(cache_v[j]),
                bsz=bs, nb=bs, tt=ss, keep=ss, rows=ss, n_pairs=2, mask_before_start=False)
            heads = lambda a: a.reshape(a.shape[0], a.shape[1], N_HEADS, head_dim)
            kp_l.append(heads(k_p))
            vp_l.append(heads(v_p))
            ks_l.append(heads(k_s))
            vs_l.append(heads(v_s))
        xp, xs = ffn_both(xp, xs, ffn2_norm[l], ffn2_w_gate[l], ffn2_w_up[l], ffn2_w_down[l], l == depth - 1)
    return (xp.reshape(bp, sp, d), xs.reshape(bs, ss, d),
            jnp.stack(conv_p), jnp.stack(conv_s),
            jnp.stack(kp_l), jnp.stack(vp_l), jnp.stack(ks_l), jnp.stack(vs_l))
```

```python
import functools

import jax
import jax.numpy as jnp
from jax import lax
from jax.experimental import pallas as pl
from jax.experimental.pallas import tpu as pltpu

CHUNK = 64
N_HEADS = 16
LEFT_CHUNKS = 8
WINDOW = LEFT_CHUNKS * CHUNK
BAND = WINDOW + CHUNK
REL_CLIP = 2 * CHUNK
N_REL = 2 * REL_CLIP + 1
CONV_W = 3
PAST_LEN = 1024
EPS = 1e-6
LOG2_E = 1.4426950408889634

V7X_LANES = 128
V7X_VMEM_LIMIT_BYTES = 56 * 1024 * 1024

F32 = jnp.float32
BF16 = jnp.bfloat16


def _params(n_grid_dims):
    return pltpu.CompilerParams(
        dimension_semantics=("arbitrary",) * n_grid_dims,
        vmem_limit_bytes=V7X_VMEM_LIMIT_BYTES,
    )


def _resident(shape):
    zeros = (0,) * len(shape)
    return pl.BlockSpec(shape, lambda *_: zeros, pipeline_mode=pl.Buffered(1))


def _rmsnorm(x, g):
    y = x * lax.rsqrt(jnp.mean(x * x, axis=-1, keepdims=True) + EPS)
    return y * g


def _ffn_kernel(x_ref, g_ref, wg_ref, wu_ref, wd_ref, gf_ref, o_ref, h_ref, *, f_chunk, final_norm):
    x = x_ref[...]
    xn = _rmsnorm(x, g_ref[...]).astype(BF16)
    d_ff = wg_ref.shape[1]
    for j in range(d_ff // f_chunk):
        sl = slice(j * f_chunk, (j + 1) * f_chunk)
        g = jnp.dot(xn, wg_ref[:, sl], preferred_element_type=F32)
        u = jnp.dot(xn, wu_ref[:, sl], preferred_element_type=F32)
        h_ref[:, sl] = (g * jax.nn.sigmoid(g) * u).astype(BF16)
    y = x + 0.5 * jnp.dot(h_ref[...], wd_ref[...], preferred_element_type=F32)
    if final_norm:
        y = _rmsnorm(y, gf_ref[...])
    o_ref[...] = y


def _ffn(x, g, wg, wu, wd, gf, *, tm, final_norm):
    t, d = x.shape
    d_ff = wg.shape[1]
    return pl.pallas_call(
        functools.partial(_ffn_kernel, f_chunk=256, final_norm=final_norm),
        grid=(t // tm,),
        in_specs=[
            pl.BlockSpec((tm, d), lambda i: (i, 0)),
            _resident((1, d)),
            _resident((d, d_ff)),
            _resident((d, d_ff)),
            _resident((d_ff, d)),
            _resident((1, d)),
        ],
        out_specs=pl.BlockSpec((tm, d), lambda i: (i, 0)),
        out_shape=jax.ShapeDtypeStruct((t, d), F32),
        scratch_shapes=[pltpu.VMEM((tm, d_ff), BF16)],
        compiler_params=_params(1),
        name="ffn",
    )(x, g, wg, wu, wd, gf)


def _conv_kernel(x_ref, st_ref, g_ref, win_ref, wdw_ref, wout_ref, o_ref, nst_ref, carry_ref, *, nb, tt):
    d = x_ref.shape[1]

    @pl.when(pl.program_id(1) == 0)
    def _():
        carry_ref[...] = st_ref[...]

    x = x_ref[...]
    xn = _rmsnorm(x, g_ref[...]).astype(BF16)
    bch = jnp.dot(xn, win_ref[...], preferred_element_type=F32)
    b, c, h = bch[:, :d], bch[:, d:2 * d], bch[:, 2 * d:]
    u = c * h
    u3 = u.reshape(nb, tt, d)
    u1 = pltpu.roll(u, 1, axis=0).reshape(nb, tt, d)
    u2 = pltpu.roll(u, 2, axis=0).reshape(nb, tt, d)
    pos = lax.broadcasted_iota(jnp.int32, (nb, tt, 1), 1)
    prev = carry_ref[...]
    p0, p1 = prev[:, 0:1, :], prev[:, 1:2, :]
    u1 = jnp.where(pos == 0, p1, u1)
    u2 = jnp.where(pos == 0, p0, jnp.where(pos == 1, p1, u2))
    w = wdw_ref[...]
    y = w[0:1, :] * u2 + w[1:2, :] * u1 + w[2:3, :] * u3
    last = u3[:, tt - (CONV_W - 1):, :]
    carry_ref[...] = last
    nst_ref[...] = last
    by = (b.reshape(nb, tt, d) * y).reshape(nb * tt, d).astype(BF16)
    o_ref[...] = x + jnp.dot(by, wout_ref[...], preferred_element_type=F32)


def _conv_mixer(x, state, g, win, wdw, wout, *, bsz, nb, tt):
    rows, d = x.shape
    nt = rows // bsz // tt
    return pl.pallas_call(
        functools.partial(_conv_kernel, nb=nb, tt=tt),
        grid=(bsz // nb, nt),
        in_specs=[
            pl.BlockSpec((nb * tt, d), lambda b, t: (b * nt + t, 0)),
            pl.BlockSpec((nb, CONV_W - 1, d), lambda b, t: (b, 0, 0)),
            _resident((1, d)),
            _resident((d, 3 * d)),
            _resident((CONV_W, d)),
            _resident((d, d)),
        ],
        out_specs=[
            pl.BlockSpec((nb * tt, d), lambda b, t: (b * nt + t, 0)),
            pl.BlockSpec((nb, CONV_W - 1, d), lambda b, t: (b, 0, 0)),
        ],
        out_shape=[
            jax.ShapeDtypeStruct((rows, d), F32),
            jax.ShapeDtypeStruct((bsz, CONV_W - 1, d), F32),
        ],
        scratch_shapes=[pltpu.VMEM((nb, CONV_W - 1, d), F32)],
        compiler_params=_params(2),
        name="conv_mixer",
    )(x, state, g, win, wdw, wout)


def _qkv_kernel(x_ref, g_ref, w_ref, kin_ref, vin_ref, q_ref, kb_ref, vb_ref, kf_ref, vf_ref,
                *, nb, tt, first_kept_tile, scale):
    del kin_ref, vin_ref
    d = x_ref.shape[1]
    xn = _rmsnorm(x_ref[...], g_ref[...]).astype(BF16)
    qkv = jnp.dot(xn, w_ref[...], preferred_element_type=F32)
    q = (qkv[:, :d] * scale).astype(BF16).reshape(nb, tt, d)
    k = qkv[:, d:2 * d].reshape(nb, tt, d)
    v = qkv[:, 2 * d:].reshape(nb, tt, d)
    kb = k.astype(BF16)
    vb = v.astype(BF16)
    for p in range(d // V7X_LANES):
        lanes = slice(p * V7X_LANES, (p + 1) * V7X_LANES)
        q_ref[:, p] = q[:, :, lanes]
        kb_ref[:, p] = kb[:, :, lanes]
        vb_ref[:, p] = vb[:, :, lanes]

    @pl.when(pl.program_id(1) >= first_kept_tile)
    def _():
        kf_ref[...] = k
        vf_ref[...] = v


def _qkv(x, g, w, k_band, v_band, *, bsz, nb, tt, keep, scale):
    rows, d = x.shape
    t_len = rows // bsz
    nt = t_len // tt
    n_pairs = d // V7X_LANES
    pad_tiles = WINDOW // tt
    first_kept_tile = nt - keep // tt
    q_spec = pl.BlockSpec((nb, n_pairs, tt, V7X_LANES), lambda b, t: (b, 0, t, 0))
    band_spec = pl.BlockSpec((nb, n_pairs, tt, V7X_LANES), lambda b, t: (b, 0, t + pad_tiles, 0))
    kept_spec = pl.BlockSpec((nb, tt, d), lambda b, t: (b, jnp.maximum(t - first_kept_tile, 0), 0))
    return pl.pallas_call(
        functools.partial(_qkv_kernel, nb=nb, tt=tt, first_kept_tile=first_kept_tile, scale=scale),
        grid=(bsz // nb, nt),
        in_specs=[
            pl.BlockSpec((nb * tt, d), lambda b, t: (b * nt + t, 0)),
            _resident((1, d)),
            _resident((d, 3 * d)),
            pl.BlockSpec(memory_space=pl.ANY),
            pl.BlockSpec(memory_space=pl.ANY),
        ],
        out_specs=[q_spec, band_spec, band_spec, kept_spec, kept_spec],
        out_shape=[
            jax.ShapeDtypeStruct((bsz, n_pairs, t_len, V7X_LANES), BF16),
            jax.ShapeDtypeStruct(k_band.shape, BF16),
            jax.ShapeDtypeStruct(v_band.shape, BF16),
            jax.ShapeDtypeStruct((bsz, keep, d), F32),
            jax.ShapeDtypeStruct((bsz, keep, d), F32),
        ],
        input_output_aliases={3: 1, 4: 2},
        compiler_params=_params(2),
        name="qkv",
    )(x, g, w, k_band, v_band)


def _bias_kernel(tab_ref, o_ref, *, var_start):
    h = pl.program_id(0)
    var_w = 2 * V7X_LANES
    ii = lax.broadcasted_iota(jnp.int32, (CHUNK, var_w), 0)
    jj = lax.broadcasted_iota(jnp.int32, (CHUNK, var_w), 1) + var_start
    idx = jnp.clip(ii + WINDOW - jj, -REL_CLIP, REL_CLIP) + REL_CLIP

    def body(r, acc):
        return jnp.where(idx == r, tab_ref[h, r] * LOG2_E, acc)

    acc = lax.fori_loop(0, N_REL, body, jnp.zeros((CHUNK, var_w), F32))
    o_ref[:, :var_start] = jnp.full((CHUNK, var_start), tab_ref[h, N_REL - 1] * LOG2_E, F32)
    o_ref[:, var_start:] = acc[:, :BAND - var_start]


def _rel_bias_band(table):
    n_heads = table.shape[0]
    var_start = WINDOW - REL_CLIP
    assert var_start % V7X_LANES == 0 and BAND - var_start <= 2 * V7X_LANES
    return pl.pallas_call(
        functools.partial(_bias_kernel, var_start=var_start),
        grid=(n_heads,),
        in_specs=[pl.BlockSpec(memory_space=pltpu.SMEM)],
        out_specs=pl.BlockSpec((None, CHUNK, BAND), lambda h: (h, 0, 0)),
        out_shape=jax.ShapeDtypeStruct((n_heads, CHUNK, BAND), F32),
        compiler_params=_params(1),
        name="rel_bias",
    )(table)


def _block_bias_t(band, rows, n_masked):
    n_heads = band.shape[0]
    n_q = rows // CHUNK
    per_chunk = [jnp.pad(band, ((0, 0), (0, 0), (c * CHUNK, (n_q - 1 - c) * CHUNK)), constant_values=-jnp.inf)
                 for c in range(n_q)]
    b = jnp.concatenate(per_chunk, axis=1)
    b = jnp.swapaxes(b.reshape(n_heads // 2, 2 * rows, WINDOW + rows), 1, 2)
    key = lax.broadcasted_iota(jnp.int32, b.shape, 1)
    return jnp.stack([jnp.where(key >= WINDOW - v * rows, b, -jnp.inf) for v in range(n_masked)] + [b])


def _attn_kernel(q_ref, k_ref, v_ref, bias_ref, o_ref, s0_ref, e0_ref, l0_ref, s1_ref, e1_ref, l1_ref,
                 *, rows, n_pairs):
    win = WINDOW + rows
    n_steps = q_ref.shape[1] // rows
    n_masked = bias_ref.shape[0] - 1
    low = lax.broadcasted_iota(jnp.int32, (1, V7X_LANES), 1) < V7X_LANES // 2

    def scores(t, s_ref):
        r0 = pl.multiple_of(t * rows, rows)
        variant = jnp.minimum(t, n_masked)
        for p in range(n_pairs):
            q = q_ref[p, pl.ds(r0, rows), :]
            zero = jnp.zeros_like(q)
            a = jnp.concatenate([jnp.where(low, q, zero), jnp.where(low, zero, q)], axis=0)
            kw = k_ref[p, pl.ds(r0, win), :]
            s = lax.dot_general(kw, a, (((1,), (1,)), ((), ())), preferred_element_type=F32)
            s_ref[p] = s + bias_ref[variant, p]

    def softmax(s_ref, e_ref, l_ref):
        for p in range(n_pairs):
            m = jnp.max(s_ref[p], axis=0, keepdims=True)
            e = jnp.exp2(s_ref[p] - m)
            l_ref[p] = jnp.sum(e, axis=0, keepdims=True)
            e_ref[p] = e.astype(BF16)

    def values(t, e_ref, l_ref):
        r0 = pl.multiple_of(t * rows, rows)
        for p in range(n_pairs):
            vw = v_ref[p, pl.ds(r0, win), :]
            ot = lax.dot_general(vw, e_ref[p], (((0,), (0,)), ((), ())), preferred_element_type=F32)
            o2 = (ot / l_ref[p]).T
            o_ref[p, pl.ds(r0, rows), :] = jnp.where(low, o2[:rows], o2[rows:]).astype(BF16)

    if n_steps == 1:
        scores(0, s0_ref)
        softmax(s0_ref, e0_ref, l0_ref)
        values(0, e0_ref, l0_ref)
        return
    assert n_steps % 2 == 0
    scores(0, s0_ref)
    scores(1, s1_ref)
    softmax(s0_ref, e0_ref, l0_ref)

    def two_steps(u, carry):
        t = 2 * u
        values(t, e0_ref, l0_ref)
        softmax(s1_ref, e1_ref, l1_ref)
        scores(t + 2, s0_ref)
        values(t + 1, e1_ref, l1_ref)
        softmax(s0_ref, e0_ref, l0_ref)
        scores(t + 3, s1_ref)
        return carry

    lax.fori_loop(0, (n_steps - 2) // 2, two_steps, 0)
    values(n_steps - 2, e0_ref, l0_ref)
    softmax(s1_ref, e1_ref, l1_ref)
    values(n_steps - 1, e1_ref, l1_ref)


def _attention(q, k_band, v_band, bias_t, *, rows, n_pairs):
    bsz, p_total, t_len, lanes = q.shape
    win = WINDOW + rows
    blk = lambda n: pl.BlockSpec((None, n_pairs, n, lanes), lambda b, g: (b, g, 0, 0))
    stage = [pltpu.VMEM((n_pairs, win, 2 * rows), F32), pltpu.VMEM((n_pairs, win, 2 * rows), BF16),
             pltpu.VMEM((n_pairs, 1, 2 * rows), F32)]
    return pl.pallas_call(
        functools.partial(_attn_kernel, rows=rows, n_pairs=n_pairs),
        grid=(bsz, p_total // n_pairs),
        in_specs=[
            blk(t_len), blk(WINDOW + t_len), blk(WINDOW + t_len),
            pl.BlockSpec((bias_t.shape[0], n_pairs, win, 2 * rows), lambda b, g: (0, g, 0, 0)),
        ],
        out_specs=blk(t_len),
        out_shape=jax.ShapeDtypeStruct(q.shape, BF16),
        scratch_shapes=stage + stage,
        compiler_params=_params(2),
        name="attention",
    )(q, k_band, v_band, bias_t)


def _oproj_kernel(x_ref, o_ref, w_ref, out_ref):
    nb, n_pairs, tt, lanes = o_ref.shape
    o = jnp.concatenate([o_ref[:, p] for p in range(n_pairs)], axis=-1).reshape(nb * tt, n_pairs * lanes)
    out_ref[...] = x_ref[...] + jnp.dot(o, w_ref[...], preferred_element_type=F32)


def _oproj(x, o, w, *, nb, tt):
    rows, d = x.shape
    bsz, n_pairs, t_len, lanes = o.shape
    nt = t_len // tt
    return pl.pallas_call(
        _oproj_kernel,
        grid=(bsz // nb, nt),
        in_specs=[
            pl.BlockSpec((nb * tt, d), lambda b, t: (b * nt + t, 0)),
            pl.BlockSpec((nb, n_pairs, tt, lanes), lambda b, t: (b, 0, t, 0)),
            _resident((d, d)),
        ],
        out_specs=pl.BlockSpec((nb * tt, d), lambda b, t: (b * nt + t, 0)),
        out_shape=jax.ShapeDtypeStruct((rows, d), F32),
        compiler_params=_params(2),
        name="oproj",
    )(x, o, w)


def _attn_mixer(x, g, wqkv, wo, band, k_band, v_band, *, bsz, nb, tt, keep, rows, n_pairs, mask_before_start):
    d = x.shape[1]
    scale = (d // N_HEADS) ** -0.5 * LOG2_E
    bias_t = _block_bias_t(band, rows, WINDOW // rows if mask_before_start else 0)
    q, k_band, v_band, k_new, v_new = _qkv(x, g, wqkv, k_band, v_band, bsz=bsz, nb=nb, tt=tt, keep=keep, scale=scale)
    o = _attention(q, k_band, v_band, bias_t, rows=rows, n_pairs=n_pairs)
    return _oproj(x, o, wo, nb=nb, tt=tt), k_new, v_new


def kernel(x_prompt, x_sample, state_conv, cache_k, cache_v, ffn1_norm, ffn1_w_gate, ffn1_w_up, ffn1_w_down, mix_norm, conv_w_in, conv_w_dw, conv_w_out, attn_w_qkv, attn_w_o, attn_rel_bias, ffn2_norm, ffn2_w_gate, ffn2_w_up, ffn2_w_down, final_norm):
    bp, sp, d = x_prompt.shape
    bs, ss, _ = x_sample.shape
    depth = ffn1_norm.shape[0]
    head_dim = d // N_HEADS
    n_pairs = d // V7X_LANES
    cache_rows = cache_k.shape[2]
    assert ss == CHUNK and cache_rows == WINDOW and PAST_LEN % CHUNK == 0
    assert sp % WINDOW == 0 and sp >= WINDOW and 2 * head_dim == V7X_LANES

    tm = 512
    rows_p = 2 * CHUNK
    xp = x_prompt.reshape(bp * sp, d)
    xs = x_sample.reshape(bs * ss, d)
    row = lambda a: a.reshape(1, d)
    gf = row(final_norm)

    def ffn_both(xp, xs, g, wg, wu, wd, final):
        g, wg, wu, wd = row(g), wg.astype(BF16), wu.astype(BF16), wd.astype(BF16)
        xp = _ffn(xp, g, wg, wu, wd, gf, tm=tm, final_norm=final)
        xs = _ffn(xs, g, wg, wu, wd, gf, tm=min(tm, xs.shape[0]), final_norm=final)
        return xp, xs

    def pair_major(a):
        return jnp.swapaxes(a.reshape(a.shape[0], a.shape[1], n_pairs, V7X_LANES), 1, 2)

    conv_p, conv_s, kp_l, vp_l, ks_l, vs_l = [], [], [], [], [], []
    for l in range(depth):
        xp, xs = ffn_both(xp, xs, ffn1_norm[l], ffn1_w_gate[l], ffn1_w_up[l], ffn1_w_down[l], False)
        j = l // 2
        g = row(mix_norm[l])
        if l % 2 == 0:
            win, wdw, wout = conv_w_in[j].astype(BF16), conv_w_dw[j], conv_w_out[j].astype(BF16)
            zero_state = jnp.zeros((bp, CONV_W - 1, d), F32)
            xp, st_p = _conv_mixer(xp, zero_state, g, win, wdw, wout, bsz=bp, nb=1, tt=tm)
            xs, st_s = _conv_mixer(xs, state_conv[j], g, win, wdw, wout, bsz=bs, nb=bs, tt=ss)
            conv_p.append(st_p)
            conv_s.append(st_s)
        else:
            wqkv, wo = attn_w_qkv[j].astype(BF16), attn_w_o[j].astype(BF16)
            band = _rel_bias_band(attn_rel_bias[j])
            zeros_p = jnp.zeros((bp, n_pairs, WINDOW + sp, V7X_LANES), BF16)
            xp, k_p, v_p = _attn_mixer(
                xp, g, wqkv, wo, band, zeros_p, zeros_p,
                bsz=bp, nb=1, tt=tm, keep=WINDOW, rows=rows_p, n_pairs=2, mask_before_start=True)
            new_rows = jnp.zeros((bs, ss, d), BF16)
            with_cache = lambda c: pair_major(jnp.concatenate(
                [c.reshape(bs, cache_rows, d).astype(BF16), new_rows], axis=1))
            xs, k_s, v_s = _attn_mixer(
                xs, g, wqkv, wo, band, with_cache(cache_k[j]), with_cache(cache_v[j]),
                bsz=bs, nb=bs, tt=ss, keep=ss, rows=ss, n_pairs=n_pairs, mask_before_start=False)
            heads = lambda a: a.reshape(a.shape[0], a.shape[1], N_HEADS, head_dim)
            kp_l.append(heads(k_p))
            vp_l.append(heads(v_p))
            ks_l.append(heads(k_s))
            vs_l.append(heads(v_s))
        xp, xs = ffn_both(xp, xs, ffn2_norm[l], ffn2_w_gate[l], ffn2_w_up[l], ffn2_w_down[l], l == depth - 1)
    return (xp.reshape(bp, sp, d), xs.reshape(bs, ss, d),
            jnp.stack(conv_p), jnp.stack(conv_s),
            jnp.stack(kp_l), jnp.stack(vp_l), jnp.stack(ks_l), jnp.stack(vs_l))
```

```python
import functools

import jax
import jax.numpy as jnp
from jax import lax
from jax.experimental import pallas as pl
from jax.experimental.pallas import tpu as pltpu

CHUNK = 64
N_HEADS = 16
LEFT_CHUNKS = 8
WINDOW = LEFT_CHUNKS * CHUNK
BAND = WINDOW + CHUNK
REL_CLIP = 2 * CHUNK
N_REL = 2 * REL_CLIP + 1
CONV_W = 3
PAST_LEN = 1024
EPS = 1e-6
LOG2_E = 1.4426950408889634

V7X_LANES = 128
V7X_VMEM_LIMIT_BYTES = 56 * 1024 * 1024

F32 = jnp.float32
BF16 = jnp.bfloat16


def _params(n_grid_dims):
    return pltpu.CompilerParams(
        dimension_semantics=("arbitrary",) * n_grid_dims,
        vmem_limit_bytes=V7X_VMEM_LIMIT_BYTES,
    )


def _resident(shape):
    zeros = (0,) * len(shape)
    return pl.BlockSpec(shape, lambda *_: zeros, pipeline_mode=pl.Buffered(1))


def _rmsnorm(x, g):
    y = x * lax.rsqrt(jnp.mean(x * x, axis=-1, keepdims=True) + EPS)
    return y * g


def _ffn_kernel(x_ref, g_ref, wg_ref, wu_ref, wd_ref, gf_ref, o_ref, h_ref, *, f_chunk, final_norm):
    x = x_ref[...]
    xn = _rmsnorm(x, g_ref[...]).astype(BF16)
    d_ff = wg_ref.shape[1]
    for j in range(d_ff // f_chunk):
        sl = slice(j * f_chunk, (j + 1) * f_chunk)
        g = jnp.dot(xn, wg_ref[:, sl], preferred_element_type=F32)
        u = jnp.dot(xn, wu_ref[:, sl], preferred_element_type=F32)
        h_ref[:, sl] = (g * jax.nn.sigmoid(g) * u).astype(BF16)
    y = x + 0.5 * jnp.dot(h_ref[...], wd_ref[...], preferred_element_type=F32)
    if final_norm:
        y = _rmsnorm(y, gf_ref[...])
    o_ref[...] = y


def _ffn(x, g, wg, wu, wd, gf, *, tm, final_norm):
    t, d = x.shape
    d_ff = wg.shape[1]
    return pl.pallas_call(
        functools.partial(_ffn_kernel, f_chunk=256, final_norm=final_norm),
        grid=(t // tm,),
        in_specs=[
            pl.BlockSpec((tm, d), lambda i: (i, 0)),
            _resident((1, d)),
            _resident((d, d_ff)),
            _resident((d, d_ff)),
            _resident((d_ff, d)),
            _resident((1, d)),
        ],
        out_specs=pl.BlockSpec((tm, d), lambda i: (i, 0)),
        out_shape=jax.ShapeDtypeStruct((t, d), F32),
        scratch_shapes=[pltpu.VMEM((tm, d_ff), BF16)],
        compiler_params=_params(1),
        name="ffn",
    )(x, g, wg, wu, wd, gf)


def _conv_kernel(x_ref, st_ref, g_ref, win_ref, wdw_ref, wout_ref, o_ref, nst_ref, carry_ref, *, nb, tt):
    d = x_ref.shape[1]

    @pl.when(pl.program_id(1) == 0)
    def _():
        carry_ref[...] = st_ref[...]

    x = x_ref[...]
    xn = _rmsnorm(x, g_ref[...]).astype(BF16)
    bch = jnp.dot(xn, win_ref[...], preferred_element_type=F32)
    b, c, h = bch[:, :d], bch[:, d:2 * d], bch[:, 2 * d:]
    u = c * h
    u3 = u.reshape(nb, tt, d)
    u1 = pltpu.roll(u, 1, axis=0).reshape(nb, tt, d)
    u2 = pltpu.roll(u, 2, axis=0).reshape(nb, tt, d)
    pos = lax.broadcasted_iota(jnp.int32, (nb, tt, 1), 1)
    prev = carry_ref[...]
    p0, p1 = prev[:, 0:1, :], prev[:, 1:2, :]
    u1 = jnp.where(pos == 0, p1, u1)
    u2 = jnp.where(pos == 0, p0, jnp.where(pos == 1, p1, u2))
    w = wdw_ref[...]
    y = w[0:1, :] * u2 + w[1:2, :] * u1 + w[2:3, :] * u3
    last = u3[:, tt - (CONV_W - 1):, :]
    carry_ref[...] = last
    nst_ref[...] = last
    by = (b.reshape(nb, tt, d) * y).reshape(nb * tt, d).astype(BF16)
    o_ref[...] = x + jnp.dot(by, wout_ref[...], preferred_element_type=F32)


def _conv_mixer(x, state, g, win, wdw, wout, *, bsz, nb, tt):
    rows, d = x.shape
    nt = rows // bsz // tt
    return pl.pallas_call(
        functools.partial(_conv_kernel, nb=nb, tt=tt),
        grid=(bsz // nb, nt),
        in_specs=[
            pl.BlockSpec((nb * tt, d), lambda b, t: (b * nt + t, 0)),
            pl.BlockSpec((nb, CONV_W - 1, d), lambda b, t: (b, 0, 0)),
            _resident((1, d)),
            _resident((d, 3 * d)),
            _resident((CONV_W, d)),
            _resident((d, d)),
        ],
        out_specs=[
            pl.BlockSpec((nb * tt, d), lambda b, t: (b * nt + t, 0)),
            pl.BlockSpec((nb, CONV_W - 1, d), lambda b, t: (b, 0, 0)),
        ],
        out_shape=[
            jax.ShapeDtypeStruct((rows, d), F32),
            jax.ShapeDtypeStruct((bsz, CONV_W - 1, d), F32),
        ],
        scratch_shapes=[pltpu.VMEM((nb, CONV_W - 1, d), F32)],
        compiler_params=_params(2),
        name="conv_mixer",
    )(x, state, g, win, wdw, wout)


def _qkv_kernel(x_ref, g_ref, w_ref, kin_ref, vin_ref, q_ref, kb_ref, vb_ref, kf_ref, vf_ref,
                *, nb, tt, first_kept_tile, scale):
    del kin_ref, vin_ref
    d = x_ref.shape[1]
    xn = _rmsnorm(x_ref[...], g_ref[...]).astype(BF16)
    qkv = jnp.dot(xn, w_ref[...], preferred_element_type=F32)
    q = (qkv[:, :d] * scale).astype(BF16).reshape(nb, tt, d)
    k = qkv[:, d:2 * d].reshape(nb, tt, d)
    v = qkv[:, 2 * d:].reshape(nb, tt, d)
    kb = k.astype(BF16)
    vb = v.astype(BF16)
    for p in range(d // V7X_LANES):
        lanes = slice(p * V7X_LANES, (p + 1) * V7X_LANES)
        q_ref[:, p] = q[:, :, lanes]
        kb_ref[:, p] = kb[:, :, lanes]
        vb_ref[:, p] = vb[:, :, lanes]

    @pl.when(pl.program_id(1) >= first_kept_tile)
    def _():
        kf_ref[...] = k
        vf_ref[...] = v


def _qkv(x, g, w, k_band, v_band, *, bsz, nb, tt, keep, scale):
    rows, d = x.shape
    t_len = rows // bsz
    nt = t_len // tt
    n_pairs = d // V7X_LANES
    pad_tiles = WINDOW // tt
    first_kept_tile = nt - keep // tt
    q_spec = pl.BlockSpec((nb, n_pairs, tt, V7X_LANES), lambda b, t: (b, 0, t, 0))
    band_spec = pl.BlockSpec((nb, n_pairs, tt, V7X_LANES), lambda b, t: (b, 0, t + pad_tiles, 0))
    kept_spec = pl.BlockSpec((nb, tt, d), lambda b, t: (b, jnp.maximum(t - first_kept_tile, 0), 0))
    return pl.pallas_call(
        functools.partial(_qkv_kernel, nb=nb, tt=tt, first_kept_tile=first_kept_tile, scale=scale),
        grid=(bsz // nb, nt),
        in_specs=[
            pl.BlockSpec((nb * tt, d), lambda b, t: (b * nt + t, 0)),
            _resident((1, d)),
            _resident((d, 3 * d)),
            pl.BlockSpec(memory_space=pl.ANY),
            pl.BlockSpec(memory_space=pl.ANY),
        ],
        out_specs=[q_spec, band_spec, band_spec, kept_spec, kept_spec],
        out_shape=[
            jax.ShapeDtypeStruct((bsz, n_pairs, t_len, V7X_LANES), BF16),
            jax.ShapeDtypeStruct(k_band.shape, BF16),
            jax.ShapeDtypeStruct(v_band.shape, BF16),
            jax.ShapeDtypeStruct((bsz, keep, d), F32),
            jax.ShapeDtypeStruct((bsz, keep, d), F32),
        ],
        input_output_aliases={3: 1, 4: 2},
        compiler_params=_params(2),
        name="qkv",
    )(x, g, w, k_band, v_band)


def _bias_kernel(tab_ref, o_ref, *, var_start):
    h = pl.program_id(0)
    var_w = 2 * V7X_LANES
    ii = lax.broadcasted_iota(jnp.int32, (CHUNK, var_w), 0)
    jj = lax.broadcasted_iota(jnp.int32, (CHUNK, var_w), 1) + var_start
    idx = jnp.clip(ii + WINDOW - jj, -REL_CLIP, REL_CLIP) + REL_CLIP

    def body(r, acc):
        return jnp.where(idx == r, tab_ref[h, r] * LOG2_E, acc)

    acc = lax.fori_loop(0, N_REL, body, jnp.zeros((CHUNK, var_w), F32))
    o_ref[:, :var_start] = jnp.full((CHUNK, var_start), tab_ref[h, N_REL - 1] * LOG2_E, F32)
    o_ref[:, var_start:] = acc[:, :BAND - var_start]


def _rel_bias_band(table):
    n_heads = table.shape[0]
    var_start = WINDOW - REL_CLIP
    assert var_start % V7X_LANES == 0 and BAND - var_start <= 2 * V7X_LANES
    return pl.pallas_call(
        functools.partial(_bias_kernel, var_start=var_start),
        grid=(n_heads,),
        in_specs=[pl.BlockSpec(memory_space=pltpu.SMEM)],
        out_specs=pl.BlockSpec((None, CHUNK, BAND), lambda h: (h, 0, 0)),
        out_shape=jax.ShapeDtypeStruct((n_heads, CHUNK, BAND), F32),
        compiler_params=_params(1),
        name="rel_bias",
    )(table)


def _block_bias_t(band, rows, n_masked):
    n_heads = band.shape[0]
    n_q = rows // CHUNK
    per_chunk = [jnp.pad(band, ((0, 0), (0, 0), (c * CHUNK, (n_q - 1 - c) * CHUNK)), constant_values=-jnp.inf)
                 for c in range(n_q)]
    b = jnp.concatenate(per_chunk, axis=1)
    b = jnp.swapaxes(b.reshape(n_heads // 2, 2 * rows, WINDOW + rows), 1, 2)
    key = lax.broadcasted_iota(jnp.int32, b.shape, 1)
    return jnp.stack([jnp.where(key >= WINDOW - v * rows, b, -jnp.inf) for v in range(n_masked)] + [b])


def _attn_kernel(q_ref, k_ref, v_ref, bias_ref, o_ref, s0_ref, e0_ref, l0_ref, s1_ref, e1_ref, l1_ref,
                 *, rows, n_pairs):
    win = WINDOW + rows
    n_steps = q_ref.shape[1] // rows
    n_masked = bias_ref.shape[0] - 1
    low = lax.broadcasted_iota(jnp.int32, (1, V7X_LANES), 1) < V7X_LANES // 2

    def scores(t, s_ref):
        r0 = pl.multiple_of(t * rows, rows)
        variant = jnp.minimum(t, n_masked)
        for p in range(n_pairs):
            q = q_ref[p, pl.ds(r0, rows), :]
            zero = jnp.zeros_like(q)
            a = jnp.concatenate([jnp.where(low, q, zero), jnp.where(low, zero, q)], axis=0)
            kw = k_ref[p, pl.ds(r0, win), :]
            s = lax.dot_general(kw, a, (((1,), (1,)), ((), ())), preferred_element_type=F32)
            s_ref[p] = s + bias_ref[variant, p]

    def softmax(s_ref, e_ref, l_ref):
        for p in range(n_pairs):
            m = jnp.max(s_ref[p], axis=0, keepdims=True)
            e = jnp.exp2(s_ref[p] - m)
            l_ref[p] = jnp.sum(e, axis=0, keepdims=True)
            e_ref[p] = e.astype(BF16)

    def values(t, e_ref, l_ref):
        r0 = pl.multiple_of(t * rows, rows)
        for p in range(n_pairs):
            vw = v_ref[p, pl.ds(r0, win), :]
            ot = lax.dot_general(vw, e_ref[p], (((0,), (0,)), ((), ())), preferred_element_type=F32)
            o2 = (ot / l_ref[p]).T
            o_ref[p, pl.ds(r0, rows), :] = jnp.where(low, o2[:rows], o2[rows:]).astype(BF16)

    if n_steps == 1:
        scores(0, s0_ref)
        softmax(s0_ref, e0_ref, l0_ref)
        values(0, e0_ref, l0_ref)
        return
    assert n_steps % 2 == 0
    scores(0, s0_ref)
    scores(1, s1_ref)
    softmax(s0_ref, e0_ref, l0_ref)

    def two_steps(u, carry):
        t = 2 * u
        values(t, e0_ref, l0_ref)
        softmax(s1_ref, e1_ref, l1_ref)
        scores(t + 2, s0_ref)
        values(t + 1, e1_ref, l1_ref)
        softmax(s0_ref, e0_ref, l0_ref)
        scores(t + 3, s1_ref)
        return carry

    lax.fori_loop(0, (n_steps - 2) // 2, two_steps, 0)
    values(n_steps - 2, e0_ref, l0_ref)
    softmax(s1_ref, e1_ref, l1_ref)
    values(n_steps - 1, e1_ref, l1_ref)


def _attention(q, k_band, v_band, bias_t, *, rows, n_pairs):
    bsz, p_total, t_len, lanes = q.shape
    win = WINDOW + rows
    blk = lambda n: pl.BlockSpec((None, n_pairs, n, lanes), lambda b, g: (b, g, 0, 0))
    stage = [pltpu.VMEM((n_pairs, win, 2 * rows), F32), pltpu.VMEM((n_pairs, win, 2 * rows), BF16),
             pltpu.VMEM((n_pairs, 1, 2 * rows), F32)]
    return pl.pallas_call(
        functools.partial(_attn_kernel, rows=rows, n_pairs=n_pairs),
        grid=(bsz, p_total // n_pairs),
        in_specs=[
            blk(t_len), blk(WINDOW + t_len), blk(WINDOW + t_len),
            pl.BlockSpec((bias_t.shape[0], n_pairs, win, 2 * rows), lambda b, g: (0, g, 0, 0)),
        ],
        out_specs=blk(t_len),
        out_shape=jax.ShapeDtypeStruct(q.shape, BF16),
        scratch_shapes=stage + stage,
        compiler_params=_params(2),
        name="attention",
    )(q, k_band, v_band, bias_t)


def _oproj_kernel(x_ref, o_ref, w_ref, out_ref):
    nb, n_pairs, tt, lanes = o_ref.shape
    o = jnp.concatenate([o_ref[:, p] for p in range(n_pairs)], axis=-1).reshape(nb * tt, n_pairs * lanes)
    out_ref[...] = x_ref[...] + jnp.dot(o, w_ref[...], preferred_element_type=F32)


def _oproj(x, o, w, *, nb, tt):
    rows, d = x.shape
    bsz, n_pairs, t_len, lanes = o.shape
    nt = t_len // tt
    return pl.pallas_call(
        _oproj_kernel,
        grid=(bsz // nb, nt),
        in_specs=[
            pl.BlockSpec((nb * tt, d), lambda b, t: (b * nt + t, 0)),
            pl.BlockSpec((nb, n_pairs, tt, lanes), lambda b, t: (b, 0, t, 0)),
            _resident((d, d)),
        ],
        out_specs=pl.BlockSpec((nb * tt, d), lambda b, t: (b * nt + t, 0)),
        out_shape=jax.ShapeDtypeStruct((rows, d), F32),
        compiler_params=_params(2),
        name="oproj",
    )(x, o, w)


def _attn_mixer(x, g, wqkv, wo, band, k_band, v_band, *, bsz, nb, tt, keep, rows, n_pairs, mask_before_start):
    d = x.shape[1]
    scale = (d // N_HEADS) ** -0.5 * LOG2_E
    bias_t = _block_bias_t(band, rows, WINDOW // rows if mask_before_start else 0)
    q, k_band, v_band, k_new, v_new = _qkv(x, g, wqkv, k_band, v_band, bsz=bsz, nb=nb, tt=tt, keep=keep, scale=scale)
    o = _attention(q, k_band, v_band, bias_t, rows=rows, n_pairs=n_pairs)
    return _oproj(x, o, wo, nb=nb, tt=tt), k_new, v_new


def kernel(x_prompt, x_sample, state_conv, cache_k, cache_v, ffn1_norm, ffn1_w_gate, ffn1_w_up, ffn1_w_down, mix_norm, conv_w_in, conv_w_dw, conv_w_out, attn_w_qkv, attn_w_o, attn_rel_bias, ffn2_norm, ffn2_w_gate, ffn2_w_up, ffn2_w_down, final_norm):
    bp, sp, d = x_prompt.shape
    bs, ss, _ = x_sample.shape
    depth = ffn1_norm.shape[0]
    head_dim = d // N_HEADS
    n_pairs = d // V7X_LANES
    cache_rows = cache_k.shape[2]
    assert ss == CHUNK and cache_rows == WINDOW and PAST_LEN % CHUNK == 0
    assert sp % WINDOW == 0 and sp >= WINDOW and 2 * head_dim == V7X_LANES

    tm = 1024
    rows_p = 2 * CHUNK
    xp = x_prompt.reshape(bp * sp, d)
    xs = x_sample.reshape(bs * ss, d)
    row = lambda a: a.reshape(1, d)
    gf = row(final_norm)

    def ffn_both(xp, xs, g, wg, wu, wd, final):
        g, wg, wu, wd = row(g), wg.astype(BF16), wu.astype(BF16), wd.astype(BF16)
        xp = _ffn(xp, g, wg, wu, wd, gf, tm=tm, final_norm=final)
        xs = _ffn(xs, g, wg, wu, wd, gf, tm=min(tm, xs.shape[0]), final_norm=final)
        return xp, xs

    def pair_major(a):
        return jnp.swapaxes(a.reshape(a.shape[0], a.shape[1], n_pairs, V7X_LANES), 1, 2)

    conv_p, conv_s, kp_l, vp_l, ks_l, vs_l = [], [], [], [], [], []
    for l in range(depth):
        xp, xs = ffn_both(xp, xs, ffn1_norm[l], ffn1_w_gate[l], ffn1_w_up[l], ffn1_w_down[l], False)
        j = l // 2
        g = row(mix_norm[l])
        if l % 2 == 0:
            win, wdw, wout = conv_w_in[j].astype(BF16), conv_w_dw[j], conv_w_out[j].astype(BF16)
            zero_state = jnp.zeros((bp, CONV_W - 1, d), F32)
            xp, st_p = _conv_mixer(xp, zero_state, g, win, wdw, wout, bsz=bp, nb=1, tt=tm)
            xs, st_s = _conv_mixer(xs, state_conv[j], g, win, wdw, wout, bsz=bs, nb=bs, tt=ss)
            conv_p.append(st_p)
            conv_s.append(st_s)
        else:
            wqkv, wo = attn_w_qkv[j].astype(BF16), attn_w_o[j].astype(BF16)
            band = _rel_bias_band(attn_rel_bias[j])
            zeros_p = jnp.zeros((bp, n_pairs, WINDOW + sp, V7X_LANES), BF16)
            xp, k_p, v_p = _attn_mixer(
                xp, g, wqkv, wo, band, zeros_p, zeros_p,
                bsz=bp, nb=1, tt=WINDOW, keep=WINDOW, rows=rows_p, n_pairs=2, mask_before_start=True)
            new_rows = jnp.zeros((bs, ss, d), BF16)
            with_cache = lambda c: pair_major(jnp.concatenate(
                [c.reshape(bs, cache_rows, d).astype(BF16), new_rows], axis=1))
            xs, k_s, v_s = _attn_mixer(
                xs, g, wqkv, wo, band, with_cache(cache_k[j]), with_cache(cache_v[j]),
                bsz=bs, nb=bs, tt=ss, keep=ss, rows=ss, n_pairs=n_pairs, mask_before_start=False)
            heads = lambda a: a.reshape(a.shape[0], a.shape[1], N_HEADS, head_dim)
            kp_l.append(heads(k_p))
            vp_l.append(heads(v_p))
            ks_l.append(heads(k_s))
            vs_l.append(heads(v_s))
        xp, xs = ffn_both(xp, xs, ffn2_norm[l], ffn2_w_gate[l], ffn2_w_up[l], ffn2_w_down[l], l == depth - 1)
    return (xp.reshape(bp, sp, d), xs.reshape(bs, ss, d),
            jnp.stack(conv_p), jnp.stack(conv_s),
            jnp.stack(kp_l), jnp.stack(vp_l), jnp.stack(ks_l), jnp.stack(vs_l))
```
